```python
import jax, jax.numpy as jnp
from jax import lax
import numpy as np

D_MODEL = 1024
BATCH = 32
SEQ = 2048
DEPTH = 4

D_RG = D_MODEL
RG_BLOCK_W = 256
RG_BLOCKS = D_RG // RG_BLOCK_W
CONV_W = 4
RG_C = 8.0
HG_EXPAND = 128
HG_HEADS = D_MODEL // HG_EXPAND
HG_DK = HG_EXPAND
HG_DV = D_MODEL // HG_HEADS
HG_CHUNK = 32
F_MIN = 1e-30
D_FF = 4 * D_MODEL
NORM_EPS = 1e-6
SPLIT_SIZES = (D_RG, D_RG, HG_HEADS * HG_DK, HG_HEADS * HG_DK, HG_HEADS * HG_DV, HG_HEADS * HG_DV, D_MODEL, D_MODEL)
D_IN = sum(SPLIT_SIZES)
SPLIT_POINTS = tuple(np.cumsum(SPLIT_SIZES)[:-1].tolist())

kernel_name = 'hybrid_rglru_hgrn2_gated_trunk'


def rms_norm(x, gain):
    xf = x.astype(jnp.float32)
    y = xf * lax.rsqrt(jnp.mean(xf * xf, axis=-1, keepdims=True) + NORM_EPS)
    return (y * gain.astype(jnp.float32)).astype(x.dtype)


def causal_depthwise_conv(x, w, b):
    y = lax.conv_general_dilated(
        x, w[:, None, :].astype(x.dtype), window_strides=(1,),
        padding=((CONV_W - 1, 0),), dimension_numbers=('NWC', 'WIO', 'NWC'),
        feature_group_count=x.shape[-1])
    return y + b.astype(x.dtype)


def rg_lru(x, w_r, b_r, w_i, b_i, lam):
    B_, S_, _ = x.shape
    xf = x.astype(jnp.float32)
    xb = xf.reshape(B_, S_, RG_BLOCKS, RG_BLOCK_W)
    r = jax.nn.sigmoid(jnp.einsum('bsnj,njk->bsnk', xb, w_r.astype(jnp.float32)).reshape(B_, S_, D_RG) + b_r.astype(jnp.float32))
    i = jax.nn.sigmoid(jnp.einsum('bsnj,njk->bsnk', xb, w_i.astype(jnp.float32)).reshape(B_, S_, D_RG) + b_i.astype(jnp.float32))
    log_a = -RG_C * r * jax.nn.softplus(-lam.astype(jnp.float32))
    a = jnp.exp(log_a)
    u = jnp.sqrt(jnp.maximum(-jnp.expm1(2.0 * log_a), 0.0)) * (i * xf)

    def combine(left, right):
        a1, b1 = left
        a2, b2 = right
        return a1 * a2, a2 * b1 + b2

    _, h = lax.associative_scan(combine, (a, u), axis=1)
    return h.astype(x.dtype)


def hgrn2_chunkwise(q, k, log_f, v):
    B_, S_, H, DK = q.shape
    DV = v.shape[-1]
    n_chunks = S_ // HG_CHUNK

    def to_chunks(t):
        return t.reshape(B_, n_chunks, HG_CHUNK, H, t.shape[-1]).transpose(1, 0, 3, 2, 4)

    causal = jnp.tril(jnp.ones((HG_CHUNK, HG_CHUNK), dtype=bool))

    def step(state, chunk):
        qc, kc, lfc, vc = chunk
        cum = jnp.cumsum(lfc, axis=2)
        o_inter = jnp.einsum('bhtk,bhkv->bhtv', qc * jnp.exp(cum), state)
        rel = cum[:, :, :, None, :] - cum[:, :, None, :, :]
        decay = jnp.where(causal[:, :, None], jnp.exp(jnp.minimum(rel, 0.0)), 0.0)
        scores = jnp.einsum('bhtsk,bhsk->bhts', qc[:, :, :, None, :] * decay, kc)
        o_intra = jnp.einsum('bhts,bhsv->bhtv', scores, vc)
        last = cum[:, :, -1, :]
        k_to_end = kc * jnp.exp(jnp.minimum(last[:, :, None, :] - cum, 0.0))
        state = jnp.exp(last)[..., None] * state + jnp.einsum('bhsk,bhsv->bhkv', k_to_end, vc)
        return state, o_inter + o_intra

    state0 = jnp.zeros((B_, H, DK, DV), jnp.float32)
    _, o = lax.scan(step, state0, (to_chunks(q), to_chunks(k), to_chunks(log_f), to_chunks(v)))
    return o.transpose(1, 0, 3, 2, 4).reshape(B_, S_, H, DV)


def hybrid_mixer(h, lower_bound, w_in, conv_w, conv_b, w_r, b_r, w_i, b_i, lam, hg_norm, w_out):
    B_, S_, _ = h.shape
    proj = h @ w_in
    xa, ga, q, f, v, g, m_a, m_b = jnp.split(proj, SPLIT_POINTS, axis=-1)
    xa = causal_depthwise_conv(xa, conv_w, conv_b)
    y_a = rg_lru(xa, w_r, b_r, w_i, b_i, lam) * jax.nn.gelu(ga)
    qf = jax.nn.silu(q.astype(jnp.float32)).reshape(B_, S_, HG_HEADS, HG_DK)
    zf = f.astype(jnp.float32).reshape(B_, S_, HG_HEADS, HG_DK)
    lb = lower_bound.reshape(HG_HEADS, HG_DK)
    sig = jax.nn.sigmoid(zf)
    f_gate = lb + (1.0 - lb) * sig
    log_f = jnp.log(jnp.maximum(f_gate, F_MIN))
    kf = (1.0 - lb) * jax.nn.sigmoid(-zf)
    vf = v.astype(jnp.float32).reshape(B_, S_, HG_HEADS, HG_DV)
    o = rms_norm(hgrn2_chunkwise(qf, kf, log_f, vf), hg_norm)
    y_b = o.reshape(B_, S_, HG_HEADS * HG_DV).astype(h.dtype) * jax.nn.silu(g)
    y = jax.nn.sigmoid(m_a) * y_a + jax.nn.sigmoid(m_b) * y_b
    return y @ w_out


def setup_inputs(seed: int = 0) -> dict:
    key = jax.random.key(seed)
    ks = jax.random.split(key, 20)
    f32 = jnp.float32
    nrm = lambda k, shape, scale: jax.random.normal(k, shape, f32) * scale
    a_c = jax.random.uniform(ks[10], (DEPTH, D_RG), f32, minval=0.9, maxval=0.999)
    a0 = a_c ** (1.0 / RG_C)
    lam = jnp.log(a0) - jnp.log1p(-a0)
    return {
        'x': jax.random.normal(ks[0], (BATCH, SEQ, D_MODEL), f32),
        'lb_logits': nrm(ks[1], (DEPTH, HG_HEADS * HG_DK), 0.1),
        'norm_mix': 1.0 + nrm(ks[2], (DEPTH, D_MODEL), 0.02),
        'w_in': nrm(ks[3], (DEPTH, D_MODEL, D_IN), D_MODEL ** -0.5),
        'conv_w': nrm(ks[4], (DEPTH, CONV_W, D_RG), CONV_W ** -0.5),
        'conv_b': nrm(ks[5], (DEPTH, D_RG), 0.02),
        'w_r': nrm(ks[6], (DEPTH, RG_BLOCKS, RG_BLOCK_W, RG_BLOCK_W), RG_BLOCK_W ** -0.5),
        'b_r': nrm(ks[7], (DEPTH, D_RG), 0.02),
        'w_i': nrm(ks[8], (DEPTH, RG_BLOCKS, RG_BLOCK_W, RG_BLOCK_W), RG_BLOCK_W ** -0.5),
        'b_i': nrm(ks[9], (DEPTH, D_RG), 0.02),
        'lam': lam,
        'hg_norm': 1.0 + nrm(ks[11], (DEPTH, HG_DV), 0.02),
        'w_out': nrm(ks[12], (DEPTH, D_MODEL, D_MODEL), D_MODEL ** -0.5),
        'norm_mlp': 1.0 + nrm(ks[13], (DEPTH, D_MODEL), 0.02),
        'w_up': nrm(ks[14], (DEPTH, D_MODEL, D_FF), D_MODEL ** -0.5),
        'w_down': nrm(ks[15], (DEPTH, D_FF, D_MODEL), D_FF ** -0.5),
        'norm_final': 1.0 + nrm(ks[16], (D_MODEL,), 0.02),
    }


def reference(x, lb_logits, norm_mix, w_in, conv_w, conv_b, w_r, b_r, w_i, b_i, lam, hg_norm, w_out, norm_mlp, w_up, w_down, norm_final):
    sm = jax.nn.softmax(lb_logits.astype(jnp.float32), axis=0)
    lower_bounds = jnp.clip(jnp.cumsum(sm, axis=0) - sm[0], 0.0, 1.0)
    for l in range(DEPTH):
        h = rms_norm(x, norm_mix[l])
        x = x + hybrid_mixer(h, lower_bounds[l], w_in[l], conv_w[l], conv_b[l], w_r[l], b_r[l],
                             w_i[l], b_i[l], lam[l], hg_norm[l], w_out[l])
        h = rms_norm(x, norm_mlp[l])
        x = x + jnp.square(jax.nn.relu(h @ w_up[l])) @ w_down[l]
    return rms_norm(x, norm_final)
```

```python
import functools

import numpy as np
import jax
import jax.numpy as jnp
from jax import lax
from jax.experimental import pallas as pl
from jax.experimental.pallas import tpu as pltpu

D_MODEL = 1024
RG_BLOCKS = 4
RG_BLOCK_W = 256
CONV_W = 4
RG_C = 8.0
HG_HEADS = 8
HG_DK = 128
HG_DV = 128
F_MIN = 1e-30
D_FF = 4096
NORM_EPS = 1e-6
N_SEG = 8

SUBLANES = 8
SEQ_TILE = 256
N_LEVELS = 8
MLP_TILE = 512
FF_CHUNK = 1024
VMEM_LIMIT_BYTES = 56 * 1024 * 1024

F32 = jnp.float32
BF16 = jnp.bfloat16


def _dot(a, b):
    return jnp.dot(a, b, preferred_element_type=F32)


def _dot_nt(a, b):
    return lax.dot_general(a, b, (((1,), (1,)), ((), ())), preferred_element_type=F32)


def _dot_tn(a, b):
    return lax.dot_general(a, b, (((0,), (0,)), ((), ())), preferred_element_type=F32)


def _sigmoid(x):
    return 1.0 / (1.0 + jnp.exp(-x))


def _rms_norm(x, gain):
    ms = jnp.mean(x * x, axis=-1, keepdims=True)
    return x * lax.rsqrt(ms + NORM_EPS) * gain


def _level_constants(tile):
    n_levels = int(np.log2(tile))
    t = np.arange(tile)[:, None]
    s = np.arange(tile)[None, :]
    mats = [(s <= t)]
    for lev in range(1, n_levels + 1):
        n = 1 << lev
        mid = (t // n) * n + n // 2
        upper = (t >= mid) & (s >= mid) & (s <= t)
        lower = (t < mid) & (s > t) & (s < mid)
        mats.append(upper | lower)
    seg = np.stack(mats).astype(np.float32)
    x = t ^ s
    lvl = np.where(t == s, 0, np.floor(np.log2(np.maximum(x, 1))).astype(np.int32) + 1)
    lvl = np.where(t < s, -1, lvl).astype(np.int32)
    return seg, lvl


def _mixer_kernel(layer, tile,
                  x_ref, lbl_ref, nrm_ref, win_ref, convw_ref, convb_ref, wr_ref, br_ref,
                  wi_ref, bi_ref, lam_ref, hgn_ref, wout_ref, seg_ref, lvl_ref,
                  o_ref,
                  tail_s, ext_s, sa_s, su_s, hcar_s, st_s, ql_s, kl_s, qd_s, ke_s, v_s, el_s, ob_s):
    n_levels = seg_ref.shape[0] - 1
    pad = SUBLANES

    @pl.when(pl.program_id(1) == 0)
    def _():
        tail_s[...] = jnp.zeros_like(tail_s)
        hcar_s[...] = jnp.zeros_like(hcar_s)
        st_s[...] = jnp.zeros_like(st_s)
        sa_s[0:pad, :] = jnp.zeros((pad, D_MODEL), F32)
        su_s[0:pad, :] = jnp.zeros((pad, D_MODEL), F32)

    x = x_ref[0]
    hb = _rms_norm(x, nrm_ref[...]).astype(BF16)

    def seg(i):
        return _dot(hb, win_ref[:, i * D_MODEL:(i + 1) * D_MODEL])

    xa = seg(0)
    ext_s[0:pad, :] = tail_s[...]
    ext_s[pad:pad + tile, :] = xa
    tail_s[...] = xa[tile - pad:tile, :]
    xc = convb_ref[...] + convw_ref[CONV_W - 1:CONV_W, :] * xa
    for j in range(CONV_W - 1):
        off = pad - (CONV_W - 1) + j
        xc = xc + convw_ref[j:j + 1, :] * ext_s[off:off + tile, :]

    xcb = xc.astype(BF16)
    r_parts, i_parts = [], []
    for n in range(RG_BLOCKS):
        blk = xcb[:, n * RG_BLOCK_W:(n + 1) * RG_BLOCK_W]
        r_parts.append(_dot(blk, wr_ref[n]))
        i_parts.append(_dot(blk, wi_ref[n]))
    r = _sigmoid(jnp.concatenate(r_parts, axis=1) + br_ref[...])
    ig = _sigmoid(jnp.concatenate(i_parts, axis=1) + bi_ref[...])
    nlam = -lam_ref[...]
    softplus = jnp.maximum(nlam, 0.0) + jnp.log1p(jnp.exp(-jnp.abs(nlam)))
    log_a = (-RG_C) * r * softplus
    a = jnp.exp(log_a)
    th = jnp.tanh(log_a)
    u = jnp.sqrt(jnp.maximum(-2.0 * th / (1.0 - th), 0.0)) * (ig * xc)

    row_in_group = lax.broadcasted_iota(jnp.int32, (tile, D_MODEL), 0) & (SUBLANES - 1)
    d = 1
    while d < SUBLANES:
        sa_s[pad:pad + tile, :] = a
        su_s[pad:pad + tile, :] = u
        keep = row_in_group >= d
        a_prev = jnp.where(keep, sa_s[pad - d:pad - d + tile, :], 1.0)
        u_prev = jnp.where(keep, su_s[pad - d:pad - d + tile, :], 0.0)
        u = u + a * u_prev
        a = a * a_prev
        d *= 2
    sa_s[pad:pad + tile, :] = a
    su_s[pad:pad + tile, :] = u
    carry = hcar_s[...]
    for g in range(tile // SUBLANES):
        lo = pad + g * SUBLANES
        hg = sa_s[lo:lo + SUBLANES, :] * carry + su_s[lo:lo + SUBLANES, :]
        su_s[lo:lo + SUBLANES, :] = hg
        carry = jnp.broadcast_to(hg[SUBLANES - 1:SUBLANES, :], (SUBLANES, D_MODEL))
    hcar_s[...] = carry
    h_rg = su_s[pad:pad + tile, :]

    ga = seg(1)
    gelu = 0.5 * ga * (1.0 + jnp.tanh(np.sqrt(2.0 / np.pi).astype(np.float32) * (ga + 0.044715 * (ga * ga * ga))))
    y = _sigmoid(seg(6)) * (h_rg * gelu)

    lbl = lbl_ref[...]
    ex = jnp.exp(lbl - jnp.max(lbl, axis=0, keepdims=True))
    sm = ex / jnp.sum(ex, axis=0, keepdims=True)
    cs = sm[0:1, :]
    for j in range(1, layer + 1):
        cs = cs + sm[j:j + 1, :]
    lb = jnp.clip(cs - sm[0:1, :], 0.0, 1.0)

    qa = seg(2)
    q = qa * _sigmoid(qa)
    z = seg(3)
    ez = jnp.exp(-jnp.abs(z))
    inv = 1.0 / (1.0 + ez)
    pos = z >= 0.0
    sig = jnp.where(pos, 1.0, ez) * inv
    nsig = jnp.where(pos, ez, 1.0) * inv
    log_f = jnp.log(jnp.maximum(lb + (1.0 - lb) * sig, F_MIN))
    k = (1.0 - lb) * nsig
    v = seg(4)

    lf_hi = log_f.astype(BF16)
    lf_lo = (log_f - lf_hi.astype(F32)).astype(BF16)

    def seg_sum(lev):
        m = seg_ref[lev]
        return _dot(m, lf_hi) + _dot(m, lf_lo)

    cum = seg_sum(0)
    last = cum[tile - 1:tile, :]
    qd = (q * jnp.exp(cum)).astype(BF16)
    ke = (k * jnp.exp(last - cum)).astype(BF16)
    vb = v.astype(BF16)
    el = jnp.broadcast_to(jnp.exp(last), (SUBLANES, D_MODEL))
    for h in range(HG_HEADS):
        cols = slice(h * HG_DK, (h + 1) * HG_DK)
        qd_s[h] = qd[:, cols]
        ke_s[h] = ke[:, cols]
        v_s[h] = vb[:, cols]
        el_s[h] = el[:, cols]
        ql_s[0, h] = q[:, cols].astype(BF16)
        kl_s[0, h] = k[:, cols].astype(BF16)
    for lev in range(1, n_levels + 1):
        e = jnp.exp(seg_sum(lev))
        qe = (q * e).astype(BF16)
        kee = (k * e).astype(BF16)
        for h in range(HG_HEADS):
            cols = slice(h * HG_DK, (h + 1) * HG_DK)
            ql_s[lev, h] = qe[:, cols]
            kl_s[lev, h] = kee[:, cols]

    lvl = lvl_ref[...]
    hgn = hgn_ref[...]

    def head_body(h, c):
        st = st_s[h]
        o = _dot_nt(qd_s[h], st.astype(BF16))
        sc = jnp.zeros((tile, tile), F32)
        for lev in range(n_levels + 1):
            sc = jnp.where(lvl == lev, _dot_nt(ql_s[lev, h], kl_s[lev, h]), sc)
        o = o + _dot(sc.astype(BF16), v_s[h])
        st_s[h] = st * el_s[h][0:1, :] + _dot_tn(v_s[h], ke_s[h])
        ob_s[h] = _rms_norm(o, hgn)
        return c

    lax.fori_loop(0, HG_HEADS, head_body, 0)

    ob = jnp.concatenate([ob_s[h] for h in range(HG_HEADS)], axis=1)
    g = seg(5)
    y = y + _sigmoid(seg(7)) * (ob * (g * _sigmoid(g)))

    o_ref[0] = x + _dot(y.astype(BF16), wout_ref[...])


def _mlp_kernel(final, x_ref, nrm_ref, wup_ref, wdn_ref, fin_ref, o_ref):
    x = x_ref[...]
    hb = _rms_norm(x, nrm_ref[...]).astype(BF16)
    acc = x
    for c in range(D_FF // FF_CHUNK):
        up = _dot(hb, wup_ref[:, c * FF_CHUNK:(c + 1) * FF_CHUNK])
        act = jnp.square(jnp.maximum(up, 0.0)).astype(BF16)
        acc = acc + _dot(act, wdn_ref[c * FF_CHUNK:(c + 1) * FF_CHUNK, :])
    if final:
        acc = _rms_norm(acc, fin_ref[...])
    o_ref[...] = acc


def _const_spec(shape):
    nd = len(shape)
    return pl.BlockSpec(shape, lambda *_: (0,) * nd, pipeline_mode=pl.Buffered(1))


def _mixer_call(layer, x, lb_logits, nrm, win, convw, convb, wr, br, wi, bi, lam, hgn, wout, seg, lvl):
    batch, seq, _ = x.shape
    tile = seg.shape[1]
    n_lev1 = seg.shape[0]
    row = lambda a: a.reshape(1, -1)
    operands = (x, lb_logits, row(nrm), win, convw, row(convb), wr, row(br), wi, row(bi), row(lam),
                row(hgn), wout, seg, lvl)
    x_spec = pl.BlockSpec((1, tile, D_MODEL), lambda b, t: (b, t, 0))
    in_specs = [x_spec] + [_const_spec(a.shape) for a in operands[1:]]
    ext = tile + SUBLANES
    scratch = [
        pltpu.VMEM((SUBLANES, D_MODEL), F32),
        pltpu.VMEM((ext, D_MODEL), F32),
        pltpu.VMEM((ext, D_MODEL), F32),
        pltpu.VMEM((ext, D_MODEL), F32),
        pltpu.VMEM((SUBLANES, D_MODEL), F32),
        pltpu.VMEM((HG_HEADS, HG_DV, HG_DK), F32),
        pltpu.VMEM((n_lev1, HG_HEADS, tile, HG_DK), BF16),
        pltpu.VMEM((n_lev1, HG_HEADS, tile, HG_DK), BF16),
        pltpu.VMEM((HG_HEADS, tile, HG_DK), BF16),
        pltpu.VMEM((HG_HEADS, tile, HG_DK), BF16),
        pltpu.VMEM((HG_HEADS, tile, HG_DV), BF16),
        pltpu.VMEM((HG_HEADS, SUBLANES, HG_DK), F32),
        pltpu.VMEM((HG_HEADS, tile, HG_DV), F32),
    ]
    return pl.pallas_call(
        functools.partial(_mixer_kernel, layer, tile),
        grid=(batch, seq // tile),
        in_specs=in_specs,
        out_specs=x_spec,
        out_shape=jax.ShapeDtypeStruct(x.shape, F32),
        scratch_shapes=scratch,
        compiler_params=pltpu.CompilerParams(
            dimension_semantics=("parallel", "arbitrary"),
            vmem_limit_bytes=VMEM_LIMIT_BYTES),
        name="mixer",
    )(*operands)


def _mlp_call(final, x2, nrm, wup, wdn, fin, tile):
    tokens = x2.shape[0]
    row = lambda a: a.reshape(1, -1)
    operands = (x2, row(nrm), wup, wdn, row(fin))
    x_spec = pl.BlockSpec((tile, D_MODEL), lambda i: (i, 0))
    in_specs = [x_spec] + [_const_spec(a.shape) for a in operands[1:]]
    return pl.pallas_call(
        functools.partial(_mlp_kernel, final),
        grid=(tokens // tile,),
        in_specs=in_specs,
        out_specs=x_spec,
        out_shape=jax.ShapeDtypeStruct(x2.shape, F32),
        compiler_params=pltpu.CompilerParams(
            dimension_semantics=("parallel",),
            vmem_limit_bytes=VMEM_LIMIT_BYTES),
        name="mlp",
    )(*operands)


def kernel(x, lb_logits, norm_mix, w_in, conv_w, conv_b, w_r, b_r, w_i, b_i, lam, hg_norm, w_out, norm_mlp, w_up, w_down, norm_final):
    batch, seq, d_model = x.shape
    depth = w_in.shape[0]
    assert d_model == D_MODEL and seq % SEQ_TILE == 0
    mlp_tile = min(MLP_TILE, batch * seq)
    assert (batch * seq) % mlp_tile == 0
    seg_np, lvl_np = _level_constants(SEQ_TILE)
    seg = jnp.asarray(seg_np, dtype=BF16)
    lvl = jnp.asarray(lvl_np)
    for l in range(depth):
        x = _mixer_call(l, x, lb_logits, norm_mix[l], w_in[l].astype(BF16), conv_w[l], conv_b[l],
                        w_r[l].astype(BF16), b_r[l], w_i[l].astype(BF16), b_i[l], lam[l],
                        hg_norm[l], w_out[l].astype(BF16), seg, lvl)
        x2 = _mlp_call(l == depth - 1, x.reshape(batch * seq, d_model), norm_mlp[l],
                       w_up[l].astype(BF16), w_down[l].astype(BF16), norm_final, mlp_tile)
        x = x2.reshape(batch, seq, d_model)
    return x
```

```python
import functools

import numpy as np
import jax
import jax.numpy as jnp
from jax import lax
from jax.experimental import pallas as pl
from jax.experimental.pallas import tpu as pltpu

D_MODEL = 1024
RG_BLOCK_W = 256
CONV_W = 4
RG_C = 8.0
HG_HEADS = 8
HG_DK = 128
HG_DV = 128
F_MIN = 1e-30
D_FF = 4096
NORM_EPS = 1e-6

SUBLANES = 8
SEQ_TILE = 256
MLP_TILE = 512
FF_CHUNK = 1024
VMEM_LIMIT_BYTES = 56 * 1024 * 1024

EXPM1_SERIES_BOUND = 2.0 ** -7
LOG2E = float(1.0 / np.log(2.0))
GELU_C = float(np.sqrt(2.0 / np.pi))
MATMUL_LEVELS = 3

F32 = jnp.float32
BF16 = jnp.bfloat16


def _dot(a, b):
    return jnp.dot(a, b, preferred_element_type=F32)


def _dot_nt(a, b):
    return lax.dot_general(a, b, (((1,), (1,)), ((), ())), preferred_element_type=F32)


def _dot_tn(a, b):
    return lax.dot_general(a, b, (((0,), (0,)), ((), ())), preferred_element_type=F32)


def _sigmoid(x):
    return pl.reciprocal(1.0 + jnp.exp(-x), approx=True)


def _rms_norm(x, gain):
    ms = jnp.mean(x * x, axis=-1, keepdims=True)
    return x * lax.rsqrt(ms + NORM_EPS) * gain


def _level_constants(tile):
    t = np.arange(tile)[:, None]
    s = np.arange(tile)[None, :]
    mats = [(s <= t)]
    for lev in range(1, MATMUL_LEVELS + 1):
        n = 1 << lev
        mid = (t // n) * n + n // 2
        upper = (t >= mid) & (s >= mid) & (s <= t)
        lower = (t < mid) & (s > t) & (s < mid)
        mats.append(upper | lower)
    seg = np.stack(mats).astype(np.float32)
    x = t ^ s
    lvl = np.where(t == s, 0, np.floor(np.log2(np.maximum(x, 1))).astype(np.int32) + 1)
    lvl = np.where(t < s, -1, lvl).astype(np.int32)
    return seg, lvl


def _mixer_kernel(layer, tile,
                  x_ref, lbl_ref, nrm_ref, win_ref, convw_ref, convb_ref, wr_ref, br_ref,
                  wi_ref, bi_ref, lam_ref, hgn_ref, wout_ref, seg_ref, lvl_ref,
                  o_ref,
                  tail_s, sa_s, su_s, hcar_s, st_s):
    n_levels = tile.bit_length() - 1
    pad = SUBLANES
    width = RG_BLOCK_W
    heads_per_block = width // HG_DK

    @pl.when(pl.program_id(1) == 0)
    def _():
        tail_s[...] = jnp.zeros_like(tail_s)
        hcar_s[...] = jnp.zeros_like(hcar_s)
        st_s[...] = jnp.zeros_like(st_s)
        sa_s[0:pad, :] = jnp.zeros((pad, D_MODEL), F32)
        su_s[0:pad, :] = jnp.zeros((pad, D_MODEL), F32)

    x = x_ref[0]
    hb = _rms_norm(x, nrm_ref[...]).astype(BF16)

    lbl = lbl_ref[...]
    ex = jnp.exp(lbl - jnp.max(lbl, axis=0, keepdims=True))
    sm = ex / jnp.sum(ex, axis=0, keepdims=True)
    cs = sm[0:1, :]
    for j in range(1, layer + 1):
        cs = cs + sm[j:j + 1, :]
    lb_all = jnp.clip(cs - sm[0:1, :], 0.0, 1.0)

    half_tile = tile // 2
    lvl_d = lvl_ref[0:half_tile, 0:half_tile]
    hgn = hgn_ref[...]
    first_row = lax.broadcasted_iota(jnp.int32, (SUBLANES, width), 0) == 0
    row = lax.broadcasted_iota(jnp.int32, (tile, width), 0)
    row_in_group = row & (SUBLANES - 1)

    def block(c):
        cols = slice(c * width, (c + 1) * width)

        def seg(i):
            return _dot(hb, win_ref[:, i * D_MODEL + c * width:i * D_MODEL + (c + 1) * width])

        xa = seg(0)

        def shift_down(p, j):
            rolled = pltpu.roll(p, 1, 0)
            head = jnp.where(first_row, tail_s[j:j + 1, cols], rolled[0:SUBLANES, :])
            tail_s[j:j + 1, cols] = rolled[0:1, :]
            return jnp.concatenate([head, rolled[SUBLANES:, :]], axis=0)

        xc = convw_ref[0:1, cols] * xa
        for j in range(1, CONV_W):
            xc = convw_ref[j:j + 1, cols] * xa + shift_down(xc, j - 1)
        xc = xc + convb_ref[:, cols]

        xcb = xc.astype(BF16)
        r = _sigmoid(_dot(xcb, wr_ref[c]) + br_ref[:, cols])
        ig = _sigmoid(_dot(xcb, wi_ref[c]) + bi_ref[:, cols])
        nlam = -lam_ref[:, cols]
        softplus = jnp.maximum(nlam, 0.0) + jnp.log1p(jnp.exp(-jnp.abs(nlam)))
        log_a = (-RG_C) * r * softplus
        a = jnp.exp(log_a)
        z2 = 2.0 * log_a
        series = -z2 * (1.0 + z2 * (1.0 / 2 + z2 * (1.0 / 6)))
        em = jnp.where(z2 > -EXPM1_SERIES_BOUND, series, 1.0 - a * a)
        root = jnp.where(em > 0.0, em * lax.rsqrt(em), 0.0)
        u = root * (ig * xc)

        d = 1
        while d < SUBLANES:
            sa_s[pad:pad + tile, cols] = a
            su_s[pad:pad + tile, cols] = u
            keep = row_in_group >= d
            a_prev = jnp.where(keep, sa_s[pad - d:pad - d + tile, cols], 1.0)
            u_prev = jnp.where(keep, su_s[pad - d:pad - d + tile, cols], 0.0)
            u = u + a * u_prev
            a = a * a_prev
            d *= 2
        carry = hcar_s[:, cols]
        h_groups = []
        for g in range(tile // SUBLANES):
            lo = g * SUBLANES
            hg = a[lo:lo + SUBLANES, :] * carry + u[lo:lo + SUBLANES, :]
            h_groups.append(hg)
            carry = jnp.broadcast_to(hg[SUBLANES - 1:SUBLANES, :], (SUBLANES, width))
        hcar_s[:, cols] = carry
        h_rg = jnp.concatenate(h_groups, axis=0)

        ga = seg(1)
        gelu = ga * _sigmoid((2.0 * GELU_C) * (ga + 0.044715 * (ga * ga * ga)))
        y = _sigmoid(seg(6)) * (h_rg * gelu)

        lb = lb_all[:, cols]
        qa = seg(2)
        q = qa * _sigmoid(qa)
        z = seg(3)
        ez = jnp.exp(-jnp.abs(z))
        inv = 1.0 / (1.0 + ez)
        pos = z >= 0.0
        sig = jnp.where(pos, 1.0, ez) * inv
        nsig = jnp.where(pos, ez, 1.0) * inv
        log_f = jnp.log(jnp.maximum(lb + (1.0 - lb) * sig, F_MIN)) * LOG2E
        k = (1.0 - lb) * nsig
        vb = seg(4).astype(BF16)

        lf_hi = log_f.astype(BF16)
        lf_lo = (log_f - lf_hi.astype(F32)).astype(BF16)

        def seg_sum(lev):
            m = seg_ref[lev]
            return _dot(m, lf_hi) + _dot(m, lf_lo)

        cum = seg_sum(0)
        last = cum[tile - 1:tile, :]
        qd = (q * jnp.exp2(cum)).astype(BF16)
        ke = (k * jnp.exp2(last - cum)).astype(BF16)
        el = jnp.exp2(last)

        level_ops = [None]
        for lev in range(1, n_levels + 1):
            half = 1 << (lev - 1)
            if lev <= MATMUL_LEVELS:
                arg = seg_sum(lev)
                src = jnp.where((row & half) != 0, q, k)
            else:
                args, srcs = [], []
                for j in range(tile // half):
                    piece = slice(j * half, (j + 1) * half)
                    mid_row = (j // 2) * 2 * half + half - 1
                    mid = jnp.broadcast_to(cum[mid_row:mid_row + 1, :], (half, width))
                    args.append(cum[piece, :] - mid if j % 2 else mid - cum[piece, :])
                    srcs.append((q if j % 2 else k)[piece, :])
                arg = jnp.concatenate(args, axis=0)
                src = jnp.concatenate(srcs, axis=0)
            level_ops.append((src * jnp.exp2(arg)).astype(BF16))
        q0 = q.astype(BF16)
        k0 = k.astype(BF16)

        ob_parts = []
        for hh in range(heads_per_block):
            h = c * heads_per_block + hh
            hs = slice(hh * HG_DK, (hh + 1) * HG_DK)
            st = st_s[h]
            o = _dot_nt(qd[:, hs], st.astype(BF16))
            diag = []
            for jb in range(2):
                rows = slice(jb * half_tile, (jb + 1) * half_tile)
                d = jnp.where(lvl_d == 0, _dot_nt(q0[rows, hs], k0[rows, hs]), 0.0)
                for lev in range(1, n_levels):
                    m = level_ops[lev][rows, hs]
                    half = 1 << (lev - 1)
                    if half < SUBLANES:
                        d = jnp.where(lvl_d == lev, _dot_nt(m, m), d)
                        continue
                    upper = [slice(j * half, (j + 1) * half) for j in range(1, half_tile // half, 2)]
                    p = _dot_nt(jnp.concatenate([m[u, :] for u in upper], axis=0), m)
                    pieces = []
                    for j in range(half_tile // half):
                        piece = d[j * half:(j + 1) * half, :]
                        if j % 2:
                            new_rows = p[(j // 2) * half:(j // 2 + 1) * half, :]
                            piece = jnp.where(lvl_d[j * half:(j + 1) * half, :] == lev, new_rows, piece)
                        pieces.append(piece)
                    d = jnp.concatenate(pieces, axis=0)
                diag.append(d)
            m_top = level_ops[n_levels]
            off = _dot_nt(m_top[half_tile:, hs], m_top[:half_tile, hs])
            o_top = _dot(diag[0].astype(BF16), vb[:half_tile, hs])
            o_bot = _dot(jnp.concatenate([off, diag[1]], axis=1).astype(BF16), vb[:, hs])
            o = o + jnp.concatenate([o_top, o_bot], axis=0)
            st_s[h] = st * el[:, hs] + _dot_tn(vb[:, hs], ke[:, hs])
            ob_parts.append(_rms_norm(o, hgn))
        ob = jnp.concatenate(ob_parts, axis=1)
        g = seg(5)
        y = y + _sigmoid(seg(7)) * (ob * (g * _sigmoid(g)))
        return _dot(y.astype(BF16), wout_ref[cols, :])

    acc = x
    for c in range(D_MODEL // width):
        acc = acc + block(c)
    o_ref[0] = acc


def _mlp_kernel(final, x_ref, nrm_ref, wup_ref, wdn_ref, fin_ref, o_ref):
    x = x_ref[...]
    hb = _rms_norm(x, nrm_ref[...]).astype(BF16)
    acc = x
    for c in range(D_FF // FF_CHUNK):
        up = _dot(hb, wup_ref[:, c * FF_CHUNK:(c + 1) * FF_CHUNK])
        act = jnp.square(jnp.maximum(up, 0.0)).astype(BF16)
        acc = acc + _dot(act, wdn_ref[c * FF_CHUNK:(c + 1) * FF_CHUNK, :])
    if final:
        acc = _rms_norm(acc, fin_ref[...])
    o_ref[...] = acc


def _const_spec(shape):
    nd = len(shape)
    return pl.BlockSpec(shape, lambda *_: (0,) * nd, pipeline_mode=pl.Buffered(1))


def _mixer_call(layer, x, lb_logits, nrm, win, convw, convb, wr, br, wi, bi, lam, hgn, wout, seg, lvl):
    batch, seq, _ = x.shape
    tile = seg.shape[1]
    row = lambda a: a.reshape(1, -1)
    operands = (x, lb_logits, row(nrm), win, convw, row(convb), wr, row(br), wi, row(bi), row(lam),
                row(hgn), wout, seg, lvl)
    x_spec = pl.BlockSpec((1, tile, D_MODEL), lambda b, t: (b, t, 0))
    in_specs = [x_spec] + [_const_spec(a.shape) for a in operands[1:]]
    ext = tile + SUBLANES
    scratch = [
        pltpu.VMEM((SUBLANES, D_MODEL), F32),
        pltpu.VMEM((ext, D_MODEL), F32),
        pltpu.VMEM((ext, D_MODEL), F32),
        pltpu.VMEM((SUBLANES, D_MODEL), F32),
        pltpu.VMEM((HG_HEADS, HG_DV, HG_DK), F32),
    ]
    return pl.pallas_call(
        functools.partial(_mixer_kernel, layer, tile),
        grid=(batch, seq // tile),
        in_specs=in_specs,
        out_specs=x_spec,
        out_shape=jax.ShapeDtypeStruct(x.shape, F32),
        scratch_shapes=scratch,
        compiler_params=pltpu.CompilerParams(
            dimension_semantics=("parallel", "arbitrary"),
            vmem_limit_bytes=VMEM_LIMIT_BYTES),
        name="mixer",
    )(*operands)


def _mlp_call(final, x2, nrm, wup, wdn, fin, tile):
    tokens = x2.shape[0]
    row = lambda a: a.reshape(1, -1)
    operands = (x2, row(nrm), wup, wdn, row(fin))
    x_spec = pl.BlockSpec((tile, D_MODEL), lambda i: (i, 0))
    in_specs = [x_spec] + [_const_spec(a.shape) for a in operands[1:]]
    return pl.pallas_call(
        functools.partial(_mlp_kernel, final),
        grid=(tokens // tile,),
        in_specs=in_specs,
        out_specs=x_spec,
        out_shape=jax.ShapeDtypeStruct(x2.shape, F32),
        compiler_params=pltpu.CompilerParams(
            dimension_semantics=("parallel",),
            vmem_limit_bytes=VMEM_LIMIT_BYTES),
        name="mlp",
    )(*operands)


def kernel(x, lb_logits, norm_mix, w_in, conv_w, conv_b, w_r, b_r, w_i, b_i, lam, hg_norm, w_out, norm_mlp, w_up, w_down, norm_final):
    batch, seq, d_model = x.shape
    depth = w_in.shape[0]
    assert d_model == D_MODEL and seq % SEQ_TILE == 0
    mlp_tile = min(MLP_TILE, batch * seq)
    assert (batch * seq) % mlp_tile == 0
    seg_np, lvl_np = _level_constants(SEQ_TILE)
    seg = jnp.asarray(seg_np, dtype=BF16)
    lvl = jnp.asarray(lvl_np)
    for l in range(depth):
        x = _mixer_call(l, x, lb_logits, norm_mix[l], w_in[l].astype(BF16), conv_w[l], conv_b[l],
                        w_r[l].astype(BF16), b_r[l], w_i[l].astype(BF16), b_i[l], lam[l],
                        hg_norm[l], w_out[l].astype(BF16), seg, lvl)
        x2 = _mlp_call(l == depth - 1, x.reshape(batch * seq, d_model), norm_mlp[l],
                       w_up[l].astype(BF16), w_down[l].astype(BF16), norm_final, mlp_tile)
        x = x2.reshape(batch, seq, d_model)
    return x
```

```python
import functools

import numpy as np
import jax
import jax.numpy as jnp
from jax import lax
from jax.experimental import pallas as pl
from jax.experimental.pallas import tpu as pltpu

D_MODEL = 1024
RG_BLOCK_W = 256
CONV_W = 4
RG_C = 8.0
HG_HEADS = 8
HG_DK = 128
HG_DV = 128
F_MIN = 1e-30
D_FF = 4096
NORM_EPS = 1e-6

SUBLANES = 8
SEQ_TILE = 256
FF_CHUNK = 1024
VMEM_LIMIT_BYTES = 60 * 1024 * 1024

EXPM1_SERIES_BOUND = 2.0 ** -7
LOG2E = float(1.0 / np.log(2.0))
GELU_C = float(np.sqrt(2.0 / np.pi))
BLOCK_STAGGER = 2
MLP_PHASE_TICKS = 2
MATMUL_LEVELS = 3

F32 = jnp.float32
BF16 = jnp.bfloat16


def _dot(a, b):
    return jnp.dot(a, b, preferred_element_type=F32)


def _dot_nt(a, b):
    return lax.dot_general(a, b, (((1,), (1,)), ((), ())), preferred_element_type=F32)


def _dot_tn(a, b):
    return lax.dot_general(a, b, (((0,), (0,)), ((), ())), preferred_element_type=F32)


def _sigmoid(x):
    return pl.reciprocal(1.0 + jnp.exp(-x), approx=True)


def _rms_norm(x, gain):
    ms = jnp.mean(x * x, axis=-1, keepdims=True)
    return x * lax.rsqrt(ms + NORM_EPS) * gain


def _level_constants(tile):
    t = np.arange(tile)[:, None]
    s = np.arange(tile)[None, :]
    mats = [(s <= t)]
    for lev in range(1, MATMUL_LEVELS + 1):
        n = 1 << lev
        mid = (t // n) * n + n // 2
        upper = (t >= mid) & (s >= mid) & (s <= t)
        lower = (t < mid) & (s > t) & (s < mid)
        mats.append(upper | lower)
    seg = np.stack(mats).astype(np.float32)
    x = t ^ s
    lvl = np.where(t == s, 0, np.floor(np.log2(np.maximum(x, 1))).astype(np.int32) + 1)
    lvl = np.where(t < s, -1, lvl).astype(np.int32)
    return seg, lvl


def _layer_kernel(layer, final, tile, tiles_per_seq,
                  x_ref, lbl_ref, nrm_ref, win_ref, convw_ref, convb_ref, wr_ref, br_ref,
                  wi_ref, bi_ref, lam_ref, hgn_ref, wout_ref, seg_ref, lvl_ref,
                  nrm2_ref, wup_ref, wdn_ref, fin_ref,
                  o_ref,
                  tail_s, sa_s, su_s, hcar_s, st_s, xn_s):
    n_levels = tile.bit_length() - 1
    pad = SUBLANES
    width = RG_BLOCK_W
    heads_per_block = width // HG_DK

    step = pl.program_id(0)

    @pl.when(step == 0)
    def _():
        xn_s[...] = jnp.zeros_like(xn_s)

    @pl.when(lax.rem(step, tiles_per_seq) == 0)
    def _():
        tail_s[...] = jnp.zeros_like(tail_s)
        hcar_s[...] = jnp.zeros_like(hcar_s)
        st_s[...] = jnp.zeros_like(st_s)
        sa_s[0:pad, :] = jnp.zeros((pad, D_MODEL), F32)
        su_s[0:pad, :] = jnp.zeros((pad, D_MODEL), F32)

    x = x_ref[0]
    hb = _rms_norm(x, nrm_ref[...]).astype(BF16)

    lbl = lbl_ref[...]
    ex = jnp.exp(lbl - jnp.max(lbl, axis=0, keepdims=True))
    sm = ex / jnp.sum(ex, axis=0, keepdims=True)
    cs = sm[0:1, :]
    for j in range(1, layer + 1):
        cs = cs + sm[j:j + 1, :]
    lb_all = jnp.clip(cs - sm[0:1, :], 0.0, 1.0)

    half_tile = tile // 2
    lvl_d = lvl_ref[0:half_tile, 0:half_tile]
    hgn = hgn_ref[...]
    first_row = lax.broadcasted_iota(jnp.int32, (SUBLANES, width), 0) == 0
    row = lax.broadcasted_iota(jnp.int32, (tile, width), 0)
    row_in_group = row & (SUBLANES - 1)

    def block(c):
        cols = slice(c * width, (c + 1) * width)

        def seg(i):
            return _dot(hb, win_ref[:, i * D_MODEL + c * width:i * D_MODEL + (c + 1) * width])

        xa = seg(0)
        yield

        def shift_down(p, j):
            rolled = pltpu.roll(p, 1, 0)
            head = jnp.where(first_row, tail_s[j:j + 1, cols], rolled[0:SUBLANES, :])
            tail_s[j:j + 1, cols] = rolled[0:1, :]
            return jnp.concatenate([head, rolled[SUBLANES:, :]], axis=0)

        xc = convw_ref[0:1, cols] * xa
        for j in range(1, CONV_W):
            xc = convw_ref[j:j + 1, cols] * xa + shift_down(xc, j - 1)
        xc = xc + convb_ref[:, cols]

        xcb = xc.astype(BF16)
        r = _sigmoid(_dot(xcb, wr_ref[c]) + br_ref[:, cols])
        ig = _sigmoid(_dot(xcb, wi_ref[c]) + bi_ref[:, cols])
        yield
        nlam = -lam_ref[:, cols]
        softplus = jnp.maximum(nlam, 0.0) + jnp.log1p(jnp.exp(-jnp.abs(nlam)))
        log_a = (-RG_C) * r * softplus
        a = jnp.exp(log_a)
        z2 = 2.0 * log_a
        series = -z2 * (1.0 + z2 * (1.0 / 2 + z2 * (1.0 / 6)))
        em = jnp.where(z2 > -EXPM1_SERIES_BOUND, series, 1.0 - a * a)
        root = jnp.where(em > 0.0, em * lax.rsqrt(em), 0.0)
        u = root * (ig * xc)
        yield

        d = 1
        while d < SUBLANES:
            sa_s[pad:pad + tile, cols] = a
            su_s[pad:pad + tile, cols] = u
            keep = row_in_group >= d
            a_prev = jnp.where(keep, sa_s[pad - d:pad - d + tile, cols], 1.0)
            u_prev = jnp.where(keep, su_s[pad - d:pad - d + tile, cols], 0.0)
            u = u + a * u_prev
            a = a * a_prev
            d *= 2
        carry = hcar_s[:, cols]
        h_groups = []
        for g in range(tile // SUBLANES):
            lo = g * SUBLANES
            hg = a[lo:lo + SUBLANES, :] * carry + u[lo:lo + SUBLANES, :]
            h_groups.append(hg)
            carry = jnp.broadcast_to(hg[SUBLANES - 1:SUBLANES, :], (SUBLANES, width))
        hcar_s[:, cols] = carry
        h_rg = jnp.concatenate(h_groups, axis=0)
        yield

        ga = seg(1)
        gelu = ga * _sigmoid((2.0 * GELU_C) * (ga + 0.044715 * (ga * ga * ga)))
        y = _sigmoid(seg(6)) * (h_rg * gelu)
        yield

        lb = lb_all[:, cols]
        qa = seg(2)
        q = qa * _sigmoid(qa)
        z = seg(3)
        ez = jnp.exp(-jnp.abs(z))
        inv = 1.0 / (1.0 + ez)
        pos = z >= 0.0
        sig = jnp.where(pos, 1.0, ez) * inv
        nsig = jnp.where(pos, ez, 1.0) * inv
        log_f = jnp.log(jnp.maximum(lb + (1.0 - lb) * sig, F_MIN)) * LOG2E
        k = (1.0 - lb) * nsig
        vb = seg(4).astype(BF16)
        yield

        lf_hi = log_f.astype(BF16)
        lf_lo = (log_f - lf_hi.astype(F32)).astype(BF16)

        def seg_sum(lev):
            m = seg_ref[lev]
            return _dot(m, lf_hi) + _dot(m, lf_lo)

        cum = seg_sum(0)
        last = cum[tile - 1:tile, :]
        qd = (q * jnp.exp2(cum)).astype(BF16)
        ke = (k * jnp.exp2(last - cum)).astype(BF16)
        el = jnp.exp2(last)
        yield

        level_ops = [None]
        for lev in range(1, n_levels + 1):
            half = 1 << (lev - 1)
            if lev <= MATMUL_LEVELS:
                arg = seg_sum(lev)
                src = jnp.where((row & half) != 0, q, k)
            else:
                args, srcs = [], []
                for j in range(tile // half):
                    piece = slice(j * half, (j + 1) * half)
                    mid_row = (j // 2) * 2 * half + half - 1
                    mid = jnp.broadcast_to(cum[mid_row:mid_row + 1, :], (half, width))
                    args.append(cum[piece, :] - mid if j % 2 else mid - cum[piece, :])
                    srcs.append((q if j % 2 else k)[piece, :])
                arg = jnp.concatenate(args, axis=0)
                src = jnp.concatenate(srcs, axis=0)
            level_ops.append((src * jnp.exp2(arg)).astype(BF16))
            if lev % 2 == 0:
                yield
        q0 = q.astype(BF16)
        k0 = k.astype(BF16)

        ob_parts = []
        for hh in range(heads_per_block):
            h = c * heads_per_block + hh
            hs = slice(hh * HG_DK, (hh + 1) * HG_DK)
            st = st_s[h]
            o = _dot_nt(qd[:, hs], st.astype(BF16))
            diag = []
            for jb in range(2):
                rows = slice(jb * half_tile, (jb + 1) * half_tile)
                d = jnp.where(lvl_d == 0, _dot_nt(q0[rows, hs], k0[rows, hs]), 0.0)
                for lev in range(1, n_levels):
                    m = level_ops[lev][rows, hs]
                    half = 1 << (lev - 1)
                    if half < SUBLANES:
                        d = jnp.where(lvl_d == lev, _dot_nt(m, m), d)
                        continue
                    upper = [slice(j * half, (j + 1) * half) for j in range(1, half_tile // half, 2)]
                    p = _dot_nt(jnp.concatenate([m[u, :] for u in upper], axis=0), m)
                    pieces = []
                    for j in range(half_tile // half):
                        piece = d[j * half:(j + 1) * half, :]
                        if j % 2:
                            new_rows = p[(j // 2) * half:(j // 2 + 1) * half, :]
                            piece = jnp.where(lvl_d[j * half:(j + 1) * half, :] == lev, new_rows, piece)
                        pieces.append(piece)
                    d = jnp.concatenate(pieces, axis=0)
                diag.append(d)
            m_top = level_ops[n_levels]
            off = _dot_nt(m_top[half_tile:, hs], m_top[:half_tile, hs])
            o_top = _dot(diag[0].astype(BF16), vb[:half_tile, hs])
            o_bot = _dot(jnp.concatenate([off, diag[1]], axis=1).astype(BF16), vb[:, hs])
            o = o + jnp.concatenate([o_top, o_bot], axis=0)
            st_s[h] = st * el[:, hs] + _dot_tn(vb[:, hs], ke[:, hs])
            ob_parts.append(_rms_norm(o, hgn))
            yield
        ob = jnp.concatenate(ob_parts, axis=1)
        g = seg(5)
        y = y + _sigmoid(seg(7)) * (ob * (g * _sigmoid(g)))
        out[c] = _dot(y.astype(BF16), wout_ref[cols, :])

    def mlp():
        xin = xn_s[lax.rem(step + 1, 2)]
        hb2 = _rms_norm(xin, nrm2_ref[...]).astype(BF16)
        acc2 = xin
        yield
        for j in range(D_FF // FF_CHUNK):
            up = _dot(hb2, wup_ref[:, j * FF_CHUNK:(j + 1) * FF_CHUNK])
            act = jnp.square(jnp.maximum(up, 0.0)).astype(BF16)
            yield
            acc2 = acc2 + _dot(act, wdn_ref[j * FF_CHUNK:(j + 1) * FF_CHUNK, :])
            yield
        if final:
            acc2 = _rms_norm(acc2, fin_ref[...])
        o_ref[0] = acc2

    n_blocks = D_MODEL // width
    out = [None] * n_blocks
    gens = [(c * BLOCK_STAGGER, 1, block(c)) for c in range(n_blocks)] + [(0, MLP_PHASE_TICKS, mlp())]
    done = [False] * len(gens)
    tick = 0
    while not all(done):
        for i, (start, every, gen) in enumerate(gens):
            if done[i] or tick < start or (tick - start) % every:
                continue
            try:
                next(gen)
            except StopIteration:
                done[i] = True
        tick += 1
    acc = x
    for c in range(n_blocks):
        acc = acc + out[c]
    xn_s[lax.rem(step, 2)] = acc


def _const_spec(shape):
    nd = len(shape)
    return pl.BlockSpec(shape, lambda *_: (0,) * nd, pipeline_mode=pl.Buffered(1))


def _layer_call(layer, final, x, lb_logits, nrm, win, convw, convb, wr, br, wi, bi, lam, hgn, wout, seg, lvl,
                nrm2, wup, wdn, fin):
    batch, seq, _ = x.shape
    tile = seg.shape[1]
    tiles_per_seq = seq // tile
    n_tiles = batch * tiles_per_seq
    row = lambda a: a.reshape(1, -1)
    operands = (x, lb_logits, row(nrm), win, convw, row(convb), wr, row(br), wi, row(bi), row(lam),
                row(hgn), wout, seg, lvl, row(nrm2), wup, wdn, row(fin))

    def tile_index(n):
        return n // tiles_per_seq, lax.rem(n, tiles_per_seq), 0

    x_spec = pl.BlockSpec((1, tile, D_MODEL), lambda n: tile_index(jnp.minimum(n, n_tiles - 1)))
    o_spec = pl.BlockSpec((1, tile, D_MODEL), lambda n: tile_index(jnp.maximum(n - 1, 0)))
    in_specs = [x_spec] + [_const_spec(a.shape) for a in operands[1:]]
    ext = tile + SUBLANES
    scratch = [
        pltpu.VMEM((SUBLANES, D_MODEL), F32),
        pltpu.VMEM((ext, D_MODEL), F32),
        pltpu.VMEM((ext, D_MODEL), F32),
        pltpu.VMEM((SUBLANES, D_MODEL), F32),
        pltpu.VMEM((HG_HEADS, HG_DV, HG_DK), F32),
        pltpu.VMEM((2, tile, D_MODEL), F32),
    ]
    return pl.pallas_call(
        functools.partial(_layer_kernel, layer, final, tile, tiles_per_seq),
        grid=(n_tiles + 1,),
        in_specs=in_specs,
        out_specs=o_spec,
        out_shape=jax.ShapeDtypeStruct(x.shape, F32),
        scratch_shapes=scratch,
        compiler_params=pltpu.CompilerParams(
            dimension_semantics=("arbitrary",),
            vmem_limit_bytes=VMEM_LIMIT_BYTES),
        name="layer",
    )(*operands)


def kernel(x, lb_logits, norm_mix, w_in, conv_w, conv_b, w_r, b_r, w_i, b_i, lam, hg_norm, w_out, norm_mlp, w_up, w_down, norm_final):
    batch, seq, d_model = x.shape
    depth = w_in.shape[0]
    assert d_model == D_MODEL and seq % SEQ_TILE == 0
    seg_np, lvl_np = _level_constants(SEQ_TILE)
    seg = jnp.asarray(seg_np, dtype=BF16)
    lvl = jnp.asarray(lvl_np)
    for l in range(depth):
        x = _layer_call(l, l == depth - 1, x, lb_logits, norm_mix[l], w_in[l].astype(BF16), conv_w[l], conv_b[l],
                        w_r[l].astype(BF16), b_r[l], w_i[l].astype(BF16), b_i[l], lam[l],
                        hg_norm[l], w_out[l].astype(BF16), seg, lvl,
                        norm_mlp[l], w_up[l].astype(BF16), w_down[l].astype(BF16), norm_final)
    return x
```

```python
import functools

import numpy as np
import jax
import jax.numpy as jnp
from jax import lax
from jax.experimental import pallas as pl
from jax.experimental.pallas import tpu as pltpu

D_MODEL = 1024
RG_BLOCK_W = 256
CONV_W = 4
RG_C = 8.0
HG_HEADS = 8
HG_DK = 128
HG_DV = 128
F_MIN = 1e-30
D_FF = 4096
NORM_EPS = 1e-6

SUBLANES = 8
SEQ_TILE = 256
FF_CHUNK = 1024
VMEM_LIMIT_BYTES = 60 * 1024 * 1024

EXPM1_SERIES_BOUND = 2.0 ** -7
LOG2E = float(1.0 / np.log(2.0))
GELU_C = float(np.sqrt(2.0 / np.pi))
BLOCK_STAGGER = 2
MLP_PHASE_TICKS = 2
VALU_LEVELS = 3

F32 = jnp.float32
BF16 = jnp.bfloat16


def _dot(a, b):
    return jnp.dot(a, b, preferred_element_type=F32)


def _dot_nt(a, b):
    return lax.dot_general(a, b, (((1,), (1,)), ((), ())), preferred_element_type=F32)


def _dot_tn(a, b):
    return lax.dot_general(a, b, (((0,), (0,)), ((), ())), preferred_element_type=F32)


def _sigmoid(x):
    return pl.reciprocal(1.0 + jnp.exp(-x), approx=True)


def _rms_norm(x, gain):
    ms = jnp.mean(x * x, axis=-1, keepdims=True)
    return x * lax.rsqrt(ms + NORM_EPS) * gain


def _pair_levels(tile):
    t = np.arange(tile)[:, None]
    s = np.arange(tile)[None, :]
    lvl = np.where(t == s, 0, np.floor(np.log2(np.maximum(t ^ s, 1))).astype(np.int32) + 1)
    return np.where(t < s, -1, lvl).astype(np.int32)


def _layer_kernel(layer, final, tile, tiles_per_seq,
                  x_ref, lbl_ref, nrm_ref, win_ref, convw_ref, convb_ref, wr_ref, br_ref,
                  wi_ref, bi_ref, lam_ref, hgn_ref, wout_ref, lvl_ref,
                  nrm2_ref, wup_ref, wdn_ref, fin_ref,
                  o_ref,
                  tail_s, sa_s, su_s, hcar_s, st_s, xn_s):
    n_levels = tile.bit_length() - 1
    pad = SUBLANES
    width = RG_BLOCK_W
    heads_per_block = width // HG_DK

    step = pl.program_id(0)

    @pl.when(step == 0)
    def _():
        xn_s[...] = jnp.zeros_like(xn_s)

    @pl.when(lax.rem(step, tiles_per_seq) == 0)
    def _():
        tail_s[...] = jnp.zeros_like(tail_s)
        hcar_s[...] = jnp.zeros_like(hcar_s)
        st_s[...] = jnp.zeros_like(st_s)
        sa_s[0:pad, :] = jnp.zeros((pad, D_MODEL), F32)
        su_s[0:pad, :] = jnp.zeros((pad, D_MODEL), F32)

    x = x_ref[0]
    hb = _rms_norm(x, nrm_ref[...]).astype(BF16)

    lbl = lbl_ref[...]
    ex = jnp.exp(lbl - jnp.max(lbl, axis=0, keepdims=True))
    sm = ex / jnp.sum(ex, axis=0, keepdims=True)
    cs = sm[0:1, :]
    for j in range(1, layer + 1):
        cs = cs + sm[j:j + 1, :]
    lb_all = jnp.clip(cs - sm[0:1, :], 0.0, 1.0)

    half_tile = tile // 2
    lvl_d = lvl_ref[0:half_tile, 0:half_tile]
    hgn = hgn_ref[...]
    first_row = lax.broadcasted_iota(jnp.int32, (SUBLANES, width), 0) == 0
    row = lax.broadcasted_iota(jnp.int32, (tile, width), 0)
    row_in_group = row & (SUBLANES - 1)
    row3 = lax.broadcasted_iota(jnp.int32, (tile // SUBLANES, SUBLANES, width), 1)

    def block(c):
        cols = slice(c * width, (c + 1) * width)

        def seg(i):
            return _dot(hb, win_ref[:, i * D_MODEL + c * width:i * D_MODEL + (c + 1) * width])

        xa = seg(0)
        yield

        def shift_down(p, j):
            rolled = pltpu.roll(p, 1, 0)
            head = jnp.where(first_row, tail_s[j:j + 1, cols], rolled[0:SUBLANES, :])
            tail_s[j:j + 1, cols] = rolled[0:1, :]
            return jnp.concatenate([head, rolled[SUBLANES:, :]], axis=0)

        xc = convw_ref[0:1, cols] * xa
        for j in range(1, CONV_W):
            xc = convw_ref[j:j + 1, cols] * xa + shift_down(xc, j - 1)
        xc = xc + convb_ref[:, cols]

        yield
        xcb = xc.astype(BF16)
        r = _sigmoid(_dot(xcb, wr_ref[c]) + br_ref[:, cols])
        ig = _sigmoid(_dot(xcb, wi_ref[c]) + bi_ref[:, cols])
        yield
        nlam = -lam_ref[:, cols]
        softplus = jnp.maximum(nlam, 0.0) + jnp.log1p(jnp.exp(-jnp.abs(nlam)))
        log_a = (-RG_C) * r * softplus
        a = jnp.exp(log_a)
        z2 = 2.0 * log_a
        series = -z2 * (1.0 + z2 * (1.0 / 2 + z2 * (1.0 / 6)))
        em = jnp.where(z2 > -EXPM1_SERIES_BOUND, series, 1.0 - a * a)
        root = jnp.where(em > 0.0, em * lax.rsqrt(em), 0.0)
        u = root * (ig * xc)
        yield

        d = 1
        while d < SUBLANES:
            sa_s[pad:pad + tile, cols] = a
            su_s[pad:pad + tile, cols] = u
            keep = row_in_group >= d
            a_prev = jnp.where(keep, sa_s[pad - d:pad - d + tile, cols], 1.0)
            u_prev = jnp.where(keep, su_s[pad - d:pad - d + tile, cols], 0.0)
            u = u + a * u_prev
            a = a * a_prev
            d *= 2
        carry = hcar_s[:, cols]
        h_groups = []
        for g in range(tile // SUBLANES):
            lo = g * SUBLANES
            hg = a[lo:lo + SUBLANES, :] * carry + u[lo:lo + SUBLANES, :]
            h_groups.append(hg)
            carry = jnp.broadcast_to(hg[SUBLANES - 1:SUBLANES, :], (SUBLANES, width))
        hcar_s[:, cols] = carry
        h_rg = jnp.concatenate(h_groups, axis=0)
        yield

        ga = seg(1)
        gelu = ga * _sigmoid((2.0 * GELU_C) * (ga + 0.044715 * (ga * ga * ga)))
        y = _sigmoid(seg(6)) * (h_rg * gelu)
        yield

        lb = lb_all[:, cols]
        qa = seg(2)
        q = qa * _sigmoid(qa)
        z = seg(3)
        ez = jnp.exp(-jnp.abs(z))
        inv = 1.0 / (1.0 + ez)
        pos = z >= 0.0
        sig = jnp.where(pos, 1.0, ez) * inv
        nsig = jnp.where(pos, ez, 1.0) * inv
        log_f = jnp.log(jnp.maximum(lb + (1.0 - lb) * sig, F_MIN)) * LOG2E
        k = (1.0 - lb) * nsig
        vb = seg(4).astype(BF16)
        yield

        lf3 = log_f.reshape(tile // SUBLANES, SUBLANES, width)

        def rot(v, d):
            return pltpu.roll(v, d % SUBLANES, 1)

        bit0, bit1, bit2 = ((row3 & b) != 0 for b in (1, 2, 4))
        s2 = lf3 + rot(lf3, 1)
        s3 = lf3 + rot(s2, 1)
        s4 = s2 + rot(s2, 2)
        n1 = rot(lf3, -1)
        n2 = rot(s2, -2)
        n3 = rot(s3, -3)
        zero = jnp.zeros_like(lf3)
        back = jnp.where(bit1, jnp.where(bit0, s4, s3), jnp.where(bit0, s2, lf3))
        ahead = jnp.where(bit1, jnp.where(bit0, zero, n1), jnp.where(bit0, n2, n3))
        arg1 = jnp.where(bit0, lf3, zero)
        arg2 = jnp.where(bit1, jnp.where(bit0, s2, lf3), jnp.where(bit0, zero, n1))
        arg3 = jnp.where(bit2, back, ahead)
        fine_args = [None] + [a3.reshape(tile, width) for a3 in (arg1, arg2, arg3)]

        prefix = jnp.where(bit2, s4 + rot(back, 4), back)
        run = jnp.zeros((1, SUBLANES, width), F32)
        offsets = []
        for g in range(tile // SUBLANES):
            offsets.append(run)
            run = run + jnp.broadcast_to(prefix[g:g + 1, SUBLANES - 1:SUBLANES, :], (1, SUBLANES, width))
        cum = (prefix + jnp.concatenate(offsets, axis=0)).reshape(tile, width)
        last = run.reshape(SUBLANES, width)[0:1, :]
        qd = (q * jnp.exp2(cum)).astype(BF16)
        ke = (k * jnp.exp2(last - cum)).astype(BF16)
        el = jnp.exp2(last)
        yield

        level_ops = [None]
        for lev in range(1, n_levels + 1):
            half = 1 << (lev - 1)
            if lev <= VALU_LEVELS:
                arg = fine_args[lev]
                src = jnp.where((row & half) != 0, q, k)
            else:
                args, srcs = [], []
                for j in range(tile // half):
                    piece = slice(j * half, (j + 1) * half)
                    mid_row = (j // 2) * 2 * half + half - 1
                    mid = jnp.broadcast_to(cum[mid_row:mid_row + 1, :], (half, width))
                    args.append(cum[piece, :] - mid if j % 2 else mid - cum[piece, :])
                    srcs.append((q if j % 2 else k)[piece, :])
                arg = jnp.concatenate(args, axis=0)
                src = jnp.concatenate(srcs, axis=0)
            level_ops.append((src * jnp.exp2(arg)).astype(BF16))
            if lev % 2 == 0:
                yield
        q0 = q.astype(BF16)
        k0 = k.astype(BF16)

        ob_parts = []
        for hh in range(heads_per_block):
            h = c * heads_per_block + hh
            hs = slice(hh * HG_DK, (hh + 1) * HG_DK)
            st = st_s[h]
            o = _dot_nt(qd[:, hs], st.astype(BF16))
            diag = []
            for jb in range(2):
                rows = slice(jb * half_tile, (jb + 1) * half_tile)
                d = jnp.where(lvl_d == 0, _dot_nt(q0[rows, hs], k0[rows, hs]), 0.0)
                for lev in range(1, n_levels):
                    m = level_ops[lev][rows, hs]
                    half = 1 << (lev - 1)
                    if half < SUBLANES:
                        d = jnp.where(lvl_d == lev, _dot_nt(m, m), d)
                        continue
                    upper = [slice(j * half, (j + 1) * half) for j in range(1, half_tile // half, 2)]
                    p = _dot_nt(jnp.concatenate([m[u, :] for u in upper], axis=0), m)
                    pieces = []
                    for j in range(half_tile // half):
                        piece = d[j * half:(j + 1) * half, :]
                        if j % 2:
                            new_rows = p[(j // 2) * half:(j // 2 + 1) * half, :]
                            piece = jnp.where(lvl_d[j * half:(j + 1) * half, :] == lev, new_rows, piece)
                        pieces.append(piece)
                    d = jnp.concatenate(pieces, axis=0)
                diag.append(d)
            yield
            m_top = level_ops[n_levels]
            off = _dot_nt(m_top[half_tile:, hs], m_top[:half_tile, hs])
            o_top = _dot(diag[0].astype(BF16), vb[:half_tile, hs])
            o_bot = _dot(jnp.concatenate([off, diag[1]], axis=1).astype(BF16), vb[:, hs])
            o = o + jnp.concatenate([o_top, o_bot], axis=0)
            st_s[h] = st * el[:, hs] + _dot_tn(vb[:, hs], ke[:, hs])
            ob_parts.append(_rms_norm(o, hgn))
            yield
        ob = jnp.concatenate(ob_parts, axis=1)
        g = seg(5)
        y = y + _sigmoid(seg(7)) * (ob * (g * _sigmoid(g)))
        out[c] = _dot(y.astype(BF16), wout_ref[cols, :])

    def mlp():
        xin = xn_s[lax.rem(step + 1, 2)]
        hb2 = _rms_norm(xin, nrm2_ref[...]).astype(BF16)
        acc2 = xin
        yield
        for j in range(D_FF // FF_CHUNK):
            up = _dot(hb2, wup_ref[:, j * FF_CHUNK:(j + 1) * FF_CHUNK])
            act = jnp.square(jnp.maximum(up, 0.0)).astype(BF16)
            yield
            acc2 = acc2 + _dot(act, wdn_ref[j * FF_CHUNK:(j + 1) * FF_CHUNK, :])
            yield
        if final:
            acc2 = _rms_norm(acc2, fin_ref[...])
        o_ref[0] = acc2

    n_blocks = D_MODEL // width
    out = [None] * n_blocks
    gens = [(c * BLOCK_STAGGER, 1, block(c)) for c in range(n_blocks)] + [(0, MLP_PHASE_TICKS, mlp())]
    done = [False] * len(gens)
    tick = 0
    while not all(done):
        for i, (start, every, gen) in enumerate(gens):
            if done[i] or tick < start or (tick - start) % every:
                continue
            try:
                next(gen)
            except StopIteration:
                done[i] = True
        tick += 1
    acc = x
    for c in range(n_blocks):
        acc = acc + out[c]
    xn_s[lax.rem(step, 2)] = acc


def _const_spec(shape):
    nd = len(shape)
    return pl.BlockSpec(shape, lambda *_: (0,) * nd, pipeline_mode=pl.Buffered(1))


def _layer_call(layer, final, x, lb_logits, nrm, win, convw, convb, wr, br, wi, bi, lam, hgn, wout, lvl,
                nrm2, wup, wdn, fin):
    batch, seq, _ = x.shape
    tile = lvl.shape[0]
    tiles_per_seq = seq // tile
    n_tiles = batch * tiles_per_seq
    row = lambda a: a.reshape(1, -1)
    operands = (x, lb_logits, row(nrm), win, convw, row(convb), wr, row(br), wi, row(bi), row(lam),
                row(hgn), wout, lvl, row(nrm2), wup, wdn, row(fin))

    def tile_index(n):
        return n // tiles_per_seq, lax.rem(n, tiles_per_seq), 0

    x_spec = pl.BlockSpec((1, tile, D_MODEL), lambda n: tile_index(jnp.minimum(n, n_tiles - 1)))
    o_spec = pl.BlockSpec((1, tile, D_MODEL), lambda n: tile_index(jnp.maximum(n - 1, 0)))
    in_specs = [x_spec] + [_const_spec(a.shape) for a in operands[1:]]
    ext = tile + SUBLANES
    scratch = [
        pltpu.VMEM((SUBLANES, D_MODEL), F32),
        pltpu.VMEM((ext, D_MODEL), F32),
        pltpu.VMEM((ext, D_MODEL), F32),
        pltpu.VMEM((SUBLANES, D_MODEL), F32),
        pltpu.VMEM((HG_HEADS, HG_DV, HG_DK), F32),
        pltpu.VMEM((2, tile, D_MODEL), F32),
    ]
    return pl.pallas_call(
        functools.partial(_layer_kernel, layer, final, tile, tiles_per_seq),
        grid=(n_tiles + 1,),
        in_specs=in_specs,
        out_specs=o_spec,
        out_shape=jax.ShapeDtypeStruct(x.shape, F32),
        scratch_shapes=scratch,
        compiler_params=pltpu.CompilerParams(
            dimension_semantics=("arbitrary",),
            vmem_limit_bytes=VMEM_LIMIT_BYTES),
        name="layer",
    )(*operands)


def kernel(x, lb_logits, norm_mix, w_in, conv_w, conv_b, w_r, b_r, w_i, b_i, lam, hg_norm, w_out, norm_mlp, w_up, w_down, norm_final):
    batch, seq, d_model = x.shape
    depth = w_in.shape[0]
    assert d_model == D_MODEL and seq % SEQ_TILE == 0
    lvl = jnp.asarray(_pair_levels(SEQ_TILE))
    for l in range(depth):
        x = _layer_call(l, l == depth - 1, x, lb_logits, norm_mix[l], w_in[l].astype(BF16), conv_w[l], conv_b[l],
                        w_r[l].astype(BF16), b_r[l], w_i[l].astype(BF16), b_i[l], lam[l],
                        hg_norm[l], w_out[l].astype(BF16), lvl,
                        norm_mlp[l], w_up[l].astype(BF16), w_down[l].astype(BF16), norm_final)
    return x
```

```python
import functools

import numpy as np
import jax
import jax.numpy as jnp
from jax import lax
from jax.experimental import pallas as pl
from jax.experimental.pallas import tpu as pltpu

D_MODEL = 1024
RG_BLOCK_W = 256
CONV_W = 4
RG_C = 8.0
HG_HEADS = 8
HG_DK = 128
HG_DV = 128
F_MIN = 1e-30
D_FF = 4096
NORM_EPS = 1e-6

SUBLANES = 8
SEQ_TILE = 256
FF_CHUNK = 2048
VMEM_LIMIT_BYTES = 60 * 1024 * 1024

EXPM1_SERIES_BOUND = 2.0 ** -7
LOG2E = float(1.0 / np.log(2.0))
GELU_C = float(np.sqrt(2.0 / np.pi))
BLOCK_STAGGER = 2
MLP_PHASE_TICKS = 2
VALU_LEVELS = 3

F32 = jnp.float32
BF16 = jnp.bfloat16


def _dot(a, b):
    return jnp.dot(a, b, preferred_element_type=F32)


def _dot_nt(a, b):
    return lax.dot_general(a, b, (((1,), (1,)), ((), ())), preferred_element_type=F32)


def _dot_tn(a, b):
    return lax.dot_general(a, b, (((0,), (0,)), ((), ())), preferred_element_type=F32)


def _sigmoid(x):
    return pl.reciprocal(1.0 + jnp.exp(-x), approx=True)


def _rms_norm(x, gain):
    ms = jnp.mean(x * x, axis=-1, keepdims=True)
    return x * lax.rsqrt(ms + NORM_EPS) * gain


def _pair_levels(tile):
    t = np.arange(tile)[:, None]
    s = np.arange(tile)[None, :]
    lvl = np.where(t == s, 0, np.floor(np.log2(np.maximum(t ^ s, 1))).astype(np.int32) + 1)
    return np.where(t < s, -1, lvl).astype(np.int32)


def _layer_kernel(layer, final, tile, tiles_per_seq,
                  x_ref, lbl_ref, nrm_ref, win_ref, convw_ref, convb_ref, wr_ref, br_ref,
                  wi_ref, bi_ref, lam_ref, hgn_ref, wout_ref, lvl_ref,
                  nrm2_ref, wup_ref, wdn_ref, fin_ref,
                  o_ref,
                  tail_s, sa_s, su_s, hcar_s, st_s, xn_s):
    n_levels = tile.bit_length() - 1
    pad = SUBLANES
    width = RG_BLOCK_W
    heads_per_block = width // HG_DK

    step = pl.program_id(0)

    @pl.when(step == 0)
    def _():
        xn_s[...] = jnp.zeros_like(xn_s)

    @pl.when(lax.rem(step, tiles_per_seq) == 0)
    def _():
        tail_s[...] = jnp.zeros_like(tail_s)
        hcar_s[...] = jnp.zeros_like(hcar_s)
        st_s[...] = jnp.zeros_like(st_s)
        sa_s[0:pad, :] = jnp.zeros((pad, D_MODEL), F32)
        su_s[0:pad, :] = jnp.zeros((pad, D_MODEL), F32)

    x = x_ref[0]
    hb = _rms_norm(x, nrm_ref[...]).astype(BF16)

    lbl = lbl_ref[...]
    ex = jnp.exp(lbl - jnp.max(lbl, axis=0, keepdims=True))
    sm = ex / jnp.sum(ex, axis=0, keepdims=True)
    cs = sm[0:1, :]
    for j in range(1, layer + 1):
        cs = cs + sm[j:j + 1, :]
    lb_all = jnp.clip(cs - sm[0:1, :], 0.0, 1.0)

    half_tile = tile // 2
    lvl_d = lvl_ref[0:half_tile, 0:half_tile]
    hgn = hgn_ref[...]
    first_row = lax.broadcasted_iota(jnp.int32, (SUBLANES, width), 0) == 0
    row = lax.broadcasted_iota(jnp.int32, (tile, width), 0)
    row_in_group = row & (SUBLANES - 1)
    row3 = lax.broadcasted_iota(jnp.int32, (tile // SUBLANES, SUBLANES, width), 1)

    def block(c):
        cols = slice(c * width, (c + 1) * width)

        def seg(i):
            return _dot(hb, win_ref[:, i * D_MODEL + c * width:i * D_MODEL + (c + 1) * width])

        xa = seg(0)
        yield

        def shift_down(p, j):
            rolled = pltpu.roll(p, 1, 0)
            head = jnp.where(first_row, tail_s[j:j + 1, cols], rolled[0:SUBLANES, :])
            tail_s[j:j + 1, cols] = rolled[0:1, :]
            return jnp.concatenate([head, rolled[SUBLANES:, :]], axis=0)

        xc = convw_ref[0:1, cols] * xa
        for j in range(1, CONV_W):
            xc = convw_ref[j:j + 1, cols] * xa + shift_down(xc, j - 1)
        xc = xc + convb_ref[:, cols]

        yield
        xcb = xc.astype(BF16)
        r = _sigmoid(_dot(xcb, wr_ref[c]) + br_ref[:, cols])
        ig = _sigmoid(_dot(xcb, wi_ref[c]) + bi_ref[:, cols])
        yield
        nlam = -lam_ref[:, cols]
        softplus = jnp.maximum(nlam, 0.0) + jnp.log1p(jnp.exp(-jnp.abs(nlam)))
        log_a = (-RG_C) * r * softplus
        a = jnp.exp(log_a)
        z2 = 2.0 * log_a
        series = -z2 * (1.0 + z2 * (1.0 / 2 + z2 * (1.0 / 6)))
        em = jnp.where(z2 > -EXPM1_SERIES_BOUND, series, 1.0 - a * a)
        root = jnp.where(em > 0.0, em * lax.rsqrt(em), 0.0)
        u = root * (ig * xc)
        yield

        d = 1
        while d < SUBLANES:
            sa_s[pad:pad + tile, cols] = a
            su_s[pad:pad + tile, cols] = u
            keep = row_in_group >= d
            a_prev = jnp.where(keep, sa_s[pad - d:pad - d + tile, cols], 1.0)
            u_prev = jnp.where(keep, su_s[pad - d:pad - d + tile, cols], 0.0)
            u = u + a * u_prev
            a = a * a_prev
            d *= 2
        carry = hcar_s[:, cols]
        h_groups = []
        for g in range(tile // SUBLANES):
            lo = g * SUBLANES
            hg = a[lo:lo + SUBLANES, :] * carry + u[lo:lo + SUBLANES, :]
            h_groups.append(hg)
            carry = jnp.broadcast_to(hg[SUBLANES - 1:SUBLANES, :], (SUBLANES, width))
        hcar_s[:, cols] = carry
        h_rg = jnp.concatenate(h_groups, axis=0)
        yield

        ga = seg(1)
        gelu = ga * _sigmoid((2.0 * GELU_C) * (ga + 0.044715 * (ga * ga * ga)))
        y = _sigmoid(seg(6)) * (h_rg * gelu)
        yield

        lb = lb_all[:, cols]
        qa = seg(2)
        q = qa * _sigmoid(qa)
        z = seg(3)
        ez = jnp.exp(-jnp.abs(z))
        inv = 1.0 / (1.0 + ez)
        pos = z >= 0.0
        sig = jnp.where(pos, 1.0, ez) * inv
        nsig = jnp.where(pos, ez, 1.0) * inv
        log_f = jnp.log(jnp.maximum(lb + (1.0 - lb) * sig, F_MIN)) * LOG2E
        k = (1.0 - lb) * nsig
        vb = seg(4).astype(BF16)
        yield

        lf3 = log_f.reshape(tile // SUBLANES, SUBLANES, width)

        def rot(v, d):
            return pltpu.roll(v, d % SUBLANES, 1)

        bit0, bit1, bit2 = ((row3 & b) != 0 for b in (1, 2, 4))
        s2 = lf3 + rot(lf3, 1)
        s3 = lf3 + rot(s2, 1)
        s4 = s2 + rot(s2, 2)
        n1 = rot(lf3, -1)
        n2 = rot(s2, -2)
        n3 = rot(s3, -3)
        zero = jnp.zeros_like(lf3)
        back = jnp.where(bit1, jnp.where(bit0, s4, s3), jnp.where(bit0, s2, lf3))
        ahead = jnp.where(bit1, jnp.where(bit0, zero, n1), jnp.where(bit0, n2, n3))
        arg1 = jnp.where(bit0, lf3, zero)
        arg2 = jnp.where(bit1, jnp.where(bit0, s2, lf3), jnp.where(bit0, zero, n1))
        arg3 = jnp.where(bit2, back, ahead)
        fine_args = [None] + [a3.reshape(tile, width) for a3 in (arg1, arg2, arg3)]

        prefix = jnp.where(bit2, s4 + rot(back, 4), back)
        run = jnp.zeros((1, SUBLANES, width), F32)
        offsets = []
        for g in range(tile // SUBLANES):
            offsets.append(run)
            run = run + jnp.broadcast_to(prefix[g:g + 1, SUBLANES - 1:SUBLANES, :], (1, SUBLANES, width))
        cum = (prefix + jnp.concatenate(offsets, axis=0)).reshape(tile, width)
        last = run.reshape(SUBLANES, width)[0:1, :]
        qd = (q * jnp.exp2(cum)).astype(BF16)
        ke = (k * jnp.exp2(last - cum)).astype(BF16)
        el = jnp.exp2(last)
        yield

        level_ops = [None]
        for lev in range(1, n_levels + 1):
            half = 1 << (lev - 1)
            if lev <= VALU_LEVELS:
                arg = fine_args[lev]
                src = jnp.where((row & half) != 0, q, k)
            else:
                args, srcs = [], []
                for j in range(tile // half):
                    piece = slice(j * half, (j + 1) * half)
                    mid_row = (j // 2) * 2 * half + half - 1
                    mid = jnp.broadcast_to(cum[mid_row:mid_row + 1, :], (half, width))
                    args.append(cum[piece, :] - mid if j % 2 else mid - cum[piece, :])
                    srcs.append((q if j % 2 else k)[piece, :])
                arg = jnp.concatenate(args, axis=0)
                src = jnp.concatenate(srcs, axis=0)
            level_ops.append((src * jnp.exp2(arg)).astype(BF16))
            if lev % 2 == 0:
                yield
        q0 = q.astype(BF16)
        k0 = k.astype(BF16)

        ob_parts = []
        for hh in range(heads_per_block):
            h = c * heads_per_block + hh
            hs = slice(hh * HG_DK, (hh + 1) * HG_DK)
            st = st_s[h]
            o = _dot_nt(qd[:, hs], st.astype(BF16))
            diag = []
            for jb in range(2):
                rows = slice(jb * half_tile, (jb + 1) * half_tile)
                d = jnp.where(lvl_d == 0, _dot_nt(q0[rows, hs], k0[rows, hs]), 0.0)
                for lev in range(1, n_levels):
                    m = level_ops[lev][rows, hs]
                    half = 1 << (lev - 1)
                    if half < SUBLANES:
                        d = jnp.where(lvl_d == lev, _dot_nt(m, m), d)
                        continue
                    upper = [slice(j * half, (j + 1) * half) for j in range(1, half_tile // half, 2)]
                    p = _dot_nt(jnp.concatenate([m[u, :] for u in upper], axis=0), m)
                    pieces = []
                    for j in range(half_tile // half):
                        piece = d[j * half:(j + 1) * half, :]
                        if j % 2:
                            new_rows = p[(j // 2) * half:(j // 2 + 1) * half, :]
                            piece = jnp.where(lvl_d[j * half:(j + 1) * half, :] == lev, new_rows, piece)
                        pieces.append(piece)
                    d = jnp.concatenate(pieces, axis=0)
                diag.append(d)
            yield
            m_top = level_ops[n_levels]
            off = _dot_nt(m_top[half_tile:, hs], m_top[:half_tile, hs])
            o_top = _dot(diag[0].astype(BF16), vb[:half_tile, hs])
            o_bot = _dot(jnp.concatenate([off, diag[1]], axis=1).astype(BF16), vb[:, hs])
            o = o + jnp.concatenate([o_top, o_bot], axis=0)
            st_s[h] = st * el[:, hs] + _dot_tn(vb[:, hs], ke[:, hs])
            ob_parts.append(_rms_norm(o, hgn))
            yield
        ob = jnp.concatenate(ob_parts, axis=1)
        g = seg(5)
        y = y + _sigmoid(seg(7)) * (ob * (g * _sigmoid(g)))
        out[c] = _dot(y.astype(BF16), wout_ref[cols, :])

    def mlp():
        xin = xn_s[lax.rem(step + 1, 2)]
        hb2 = _rms_norm(xin, nrm2_ref[...]).astype(BF16)
        acc2 = xin
        yield
        for j in range(D_FF // FF_CHUNK):
            up = _dot(hb2, wup_ref[:, j * FF_CHUNK:(j + 1) * FF_CHUNK])
            act = jnp.square(jnp.maximum(up, 0.0)).astype(BF16)
            yield
            acc2 = acc2 + _dot(act, wdn_ref[j * FF_CHUNK:(j + 1) * FF_CHUNK, :])
            yield
        if final:
            acc2 = _rms_norm(acc2, fin_ref[...])
        o_ref[0] = acc2

    n_blocks = D_MODEL // width
    out = [None] * n_blocks
    gens = [(c * BLOCK_STAGGER, 1, block(c)) for c in range(n_blocks)] + [(0, MLP_PHASE_TICKS, mlp())]
    done = [False] * len(gens)
    tick = 0
    while not all(done):
        for i, (start, every, gen) in enumerate(gens):
            if done[i] or tick < start or (tick - start) % every:
                continue
            try:
                next(gen)
            except StopIteration:
                done[i] = True
        tick += 1
    acc = x
    for c in range(n_blocks):
        acc = acc + out[c]
    xn_s[lax.rem(step, 2)] = acc


def _const_spec(shape):
    nd = len(shape)
    return pl.BlockSpec(shape, lambda *_: (0,) * nd, pipeline_mode=pl.Buffered(1))


def _layer_call(layer, final, x, lb_logits, nrm, win, convw, convb, wr, br, wi, bi, lam, hgn, wout, lvl,
                nrm2, wup, wdn, fin):
    batch, seq, _ = x.shape
    tile = lvl.shape[0]
    tiles_per_seq = seq // tile
    n_tiles = batch * tiles_per_seq
    row = lambda a: a.reshape(1, -1)
    operands = (x, lb_logits, row(nrm), win, convw, row(convb), wr, row(br), wi, row(bi), row(lam),
                row(hgn), wout, lvl, row(nrm2), wup, wdn, row(fin))

    def tile_index(n):
        return n // tiles_per_seq, lax.rem(n, tiles_per_seq), 0

    x_spec = pl.BlockSpec((1, tile, D_MODEL), lambda n: tile_index(jnp.minimum(n, n_tiles - 1)))
    o_spec = pl.BlockSpec((1, tile, D_MODEL), lambda n: tile_index(jnp.maximum(n - 1, 0)))
    in_specs = [x_spec] + [_const_spec(a.shape) for a in operands[1:]]
    ext = tile + SUBLANES
    scratch = [
        pltpu.VMEM((SUBLANES, D_MODEL), F32),
        pltpu.VMEM((ext, D_MODEL), F32),
        pltpu.VMEM((ext, D_MODEL), F32),
        pltpu.VMEM((SUBLANES, D_MODEL), F32),
        pltpu.VMEM((HG_HEADS, HG_DV, HG_DK), F32),
        pltpu.VMEM((2, tile, D_MODEL), F32),
    ]
    return pl.pallas_call(
        functools.partial(_layer_kernel, layer, final, tile, tiles_per_seq),
        grid=(n_tiles + 1,),
        in_specs=in_specs,
        out_specs=o_spec,
        out_shape=jax.ShapeDtypeStruct(x.shape, F32),
        scratch_shapes=scratch,
        compiler_params=pltpu.CompilerParams(
            dimension_semantics=("arbitrary",),
            vmem_limit_bytes=VMEM_LIMIT_BYTES),
        name="layer",
    )(*operands)


def kernel(x, lb_logits, norm_mix, w_in, conv_w, conv_b, w_r, b_r, w_i, b_i, lam, hg_norm, w_out, norm_mlp, w_up, w_down, norm_final):
    batch, seq, d_model = x.shape
    depth = w_in.shape[0]
    assert d_model == D_MODEL and seq % SEQ_TILE == 0
    lvl = jnp.asarray(_pair_levels(SEQ_TILE))
    for l in range(depth):
        x = _layer_call(l, l == depth - 1, x, lb_logits, norm_mix[l], w_in[l].astype(BF16), conv_w[l], conv_b[l],
                        w_r[l].astype(BF16), b_r[l], w_i[l].astype(BF16), b_i[l], lam[l],
                        hg_norm[l], w_out[l].astype(BF16), lvl,
                        norm_mlp[l], w_up[l].astype(BF16), w_down[l].astype(BF16), norm_final)
    return x
```
